```python
import math
import jax, jax.numpy as jnp
from jax import lax
import numpy as np

D_MODEL = 1024
BATCH = 8
SEQ = 2048
DEPTH = 1
DEC_BATCH = 128
DEC_SEQ = 4
PAST_LEN = 16384
PAGE_SIZE = 128

N_META = 16
S5_WIDTH = D_MODEL // 2
S5_GROUP = 16
S5_GROUPS = S5_WIDTH // S5_GROUP
S5_STATE = 64
GDN_HEADS = 4
GDN_DK = 128
GDN_DV = 128
GDN_KW = GDN_HEADS * GDN_DK
GDN_VW = GDN_HEADS * GDN_DV
QKV_W = 2 * GDN_KW + GDN_VW
CONV_W = 4
CHUNK = 64
D_FF = ((8 * D_MODEL + 3 * 256 - 1) // (3 * 256)) * 256
IN_W = S5_WIDTH + QKV_W + GDN_VW + 2 * GDN_HEADS + 2 * D_MODEL
EPS = 1e-6

kernel_name = "hybrid_s5_gated_deltanet_step"

F32 = jnp.float32


def rms_norm(x, g):
    xf = x.astype(F32)
    y = xf * lax.rsqrt(jnp.mean(xf * xf, axis=-1, keepdims=True) + EPS)
    return (y * g.astype(F32)).astype(x.dtype)


def l2_norm(x):
    return x * lax.rsqrt(jnp.sum(x * x, axis=-1, keepdims=True) + EPS)


def s5_mix(u, h0_re, h0_im, A_re, A_im, B_re, B_im, C_re, C_im, log_dt, D):
    b, l, _ = u.shape
    uf = u.astype(F32)
    ug = uf.reshape(b, l, S5_GROUPS, S5_GROUP).astype(jnp.complex64)
    lam = lax.complex(A_re.astype(F32), A_im.astype(F32))
    dt = jnp.exp(log_dt.astype(F32))[:, None]
    a_bar = jnp.exp(lam * dt)
    B = lax.complex(B_re.astype(F32), B_im.astype(F32))
    B_bar = ((a_bar - 1.0) / lam)[..., None] * B
    bu = jnp.einsum('gpc,blgc->blgp', B_bar, ug)
    h0 = lax.complex(h0_re.astype(F32), h0_im.astype(F32))
    bu = bu.at[:, 0].add(a_bar[None] * h0)
    a_seq = jnp.broadcast_to(a_bar, bu.shape)

    def combine(e1, e2):
        a1, b1 = e1
        a2, b2 = e2
        return a1 * a2, a2 * b1 + b2

    _, h = lax.associative_scan(combine, (a_seq, bu), axis=1)
    C = lax.complex(C_re.astype(F32), C_im.astype(F32))
    y = jnp.real(jnp.einsum('gcp,blgp->blgc', C, h)).reshape(b, l, S5_WIDTH) + D.astype(F32) * uf
    h_last = h[:, -1]
    return y.astype(u.dtype), jnp.real(h_last), jnp.imag(h_last)


def short_conv(x, buf, w):
    l = x.shape[1]
    xp = jnp.concatenate([buf.astype(x.dtype), x], axis=1)
    y = w[0] * xp[:, 0:l]
    for j in range(1, CONV_W):
        y = y + w[j] * xp[:, j:j + l]
    return jax.nn.silu(y), xp[:, xp.shape[1] - (CONV_W - 1):]


def gdn_segment(q, k, v, g, beta, S0):
    b, l, h, dk = q.shape
    dv = v.shape[-1]
    c = min(CHUNK, l)
    n = -(-l // c)
    pad = n * c - l
    if pad:
        padf = lambda t: jnp.pad(t, [(0, 0), (0, pad)] + [(0, 0)] * (t.ndim - 2))
        q, k, v, g, beta = padf(q), padf(k), padf(v), padf(g), padf(beta)

    def to_chunks(t):
        return jnp.moveaxis(t.reshape((b, n, c) + t.shape[2:]), 3, 1)

    q, k, v, g, beta = to_chunks(q), to_chunks(k), to_chunks(v), to_chunks(g), to_chunks(beta)
    gc = jnp.cumsum(g, axis=-1)
    idx = jnp.arange(c)
    causal = idx[:, None] >= idx[None, :]
    strict = idx[:, None] > idx[None, :]
    decay = jnp.exp(jnp.where(causal, gc[..., :, None] - gc[..., None, :], -jnp.inf))
    kb = k * beta[..., None]
    m = jnp.where(strict, jnp.einsum('bhnid,bhnjd->bhnij', kb, k) * decay, 0.0)
    eye = jnp.eye(c, dtype=F32)
    t = lax.linalg.triangular_solve(eye + m, jnp.broadcast_to(eye, m.shape), left_side=True, lower=True)
    u = t @ (v * beta[..., None])
    w = t @ (kb * jnp.exp(gc)[..., None])
    qk = jnp.einsum('bhnid,bhnjd->bhnij', q, k) * decay
    q_dec = q * jnp.exp(gc)[..., None]
    g_last = gc[..., -1]
    k_dec = k * jnp.exp(g_last[..., None] - gc)[..., None]

    def step(S, xs):
        u_i, w_i, qk_i, qd_i, kd_i, gl_i = xs
        v_new = u_i - w_i @ S
        o = qd_i @ S + qk_i @ v_new
        S = S * jnp.exp(gl_i)[..., None, None] + jnp.einsum('bhcd,bhce->bhde', kd_i, v_new)
        return S, o

    xs = (jnp.moveaxis(u, 2, 0), jnp.moveaxis(w, 2, 0), jnp.moveaxis(qk, 2, 0),
          jnp.moveaxis(q_dec, 2, 0), jnp.moveaxis(k_dec, 2, 0), jnp.moveaxis(g_last, 2, 0))
    S, o = lax.scan(step, S0, xs)
    o = jnp.transpose(o, (1, 0, 3, 2, 4)).reshape(b, n * c, h, dv)[:, :l]
    return o, S


def layer(x, h_re, h_im, s_gdn, conv_buf, splits, lp):
    b, l, _ = x.shape
    hn = rms_norm(x, lp['norm_mix_pre'])
    proj = hn @ lp['w_in']
    o1 = S5_WIDTH
    o2 = o1 + QKV_W
    o3 = o2 + GDN_VW
    o4 = o3 + GDN_HEADS
    o5 = o4 + GDN_HEADS
    u, qkv, z, a, bt, gates = jnp.split(proj, [o1, o2, o3, o4, o5], axis=-1)
    y_s5, h_re_n, h_im_n = s5_mix(u, h_re, h_im, lp['s5_A_re'], lp['s5_A_im'], lp['s5_B_re'], lp['s5_B_im'],
                                  lp['s5_C_re'], lp['s5_C_im'], lp['s5_log_dt'], lp['s5_D'])
    gs = jax.nn.gelu(y_s5)
    br_a = (gs @ lp['w_s5_glu_a']) * jax.nn.sigmoid(gs @ lp['w_s5_glu_b'])
    qkv_c, conv_new = short_conv(qkv, conv_buf, lp['gdn_conv_w'])
    qkv_c = qkv_c.astype(F32)
    q = l2_norm(qkv_c[..., :GDN_KW].reshape(b, l, GDN_HEADS, GDN_DK)) * (GDN_DK ** -0.5)
    k = l2_norm(qkv_c[..., GDN_KW:2 * GDN_KW].reshape(b, l, GDN_HEADS, GDN_DK))
    v = qkv_c[..., 2 * GDN_KW:].reshape(b, l, GDN_HEADS, GDN_DV)
    g = -jnp.exp(lp['gdn_A_log'].astype(F32)) * jax.nn.softplus(a.astype(F32) + lp['gdn_dt_bias'].astype(F32))
    beta = jax.nn.sigmoid(bt.astype(F32))
    S = s_gdn.astype(F32)
    bounds = (0,) + tuple(splits) + (l,)
    outs = []
    for s0, s1 in zip(bounds[:-1], bounds[1:]):
        o_seg, S = gdn_segment(q[:, s0:s1], k[:, s0:s1], v[:, s0:s1], g[:, s0:s1], beta[:, s0:s1], S)
        outs.append(o_seg)
    o = jnp.concatenate(outs, axis=1)
    zf = z.astype(F32).reshape(b, l, GDN_HEADS, GDN_DV)
    o = rms_norm(o, lp['gdn_norm']) * jax.nn.silu(zf)
    br_b = o.reshape(b, l, GDN_VW).astype(x.dtype) @ lp['w_gdn_out']
    ga, gb = jnp.split(gates, 2, axis=-1)
    mix = (jax.nn.sigmoid(ga) * br_a + jax.nn.sigmoid(gb) * br_b) @ lp['w_out']
    x = x + rms_norm(mix, lp['norm_mix_post'])
    f = rms_norm(x, lp['norm_ffn_pre'])
    f = (jax.nn.silu(f @ lp['w_ffn_gate']) * (f @ lp['w_ffn_up'])) @ lp['w_ffn_down']
    x = x + rms_norm(f, lp['norm_ffn_post'])
    return x, h_re_n, h_im_n, S, conv_new


def setup_inputs(seed: int = 0) -> dict:
    key = jax.random.key(seed)
    ks = jax.random.split(key, 40)
    nrm = lambda i, shape, s=1.0: jax.random.normal(ks[i], shape, F32) * s
    gain = lambda i, shape: 1.0 + 0.02 * jax.random.normal(ks[i], shape, F32)
    n_idx = jnp.arange(S5_STATE, dtype=F32)
    dt_gdn = jnp.exp(jax.random.uniform(ks[30], (DEPTH, GDN_HEADS), F32, math.log(0.001), math.log(0.1)))
    return {
        'x_prompt': nrm(0, (BATCH, SEQ, D_MODEL)),
        'x_sample': nrm(1, (DEC_BATCH, DEC_SEQ, D_MODEL)),
        'state_s5_re': nrm(2, (DEPTH, DEC_BATCH, S5_GROUPS, S5_STATE), 0.1),
        'state_s5_im': nrm(3, (DEPTH, DEC_BATCH, S5_GROUPS, S5_STATE), 0.1),
        'state_gdn': nrm(4, (DEPTH, DEC_BATCH, GDN_HEADS, GDN_DK, GDN_DV), 0.1),
        'state_conv': nrm(5, (DEPTH, DEC_BATCH, CONV_W - 1, QKV_W)),
        'meta_tokens': nrm(6, (N_META, D_MODEL)),
        'norm_mix_pre': gain(7, (DEPTH, D_MODEL)),
        'norm_mix_post': gain(8, (DEPTH, D_MODEL)),
        'norm_ffn_pre': gain(9, (DEPTH, D_MODEL)),
        'norm_ffn_post': gain(10, (DEPTH, D_MODEL)),
        'w_in': nrm(11, (DEPTH, D_MODEL, IN_W), D_MODEL ** -0.5),
        's5_A_re': -0.5 + nrm(12, (DEPTH, S5_GROUPS, S5_STATE), 0.01),
        's5_A_im': math.pi * n_idx + nrm(13, (DEPTH, S5_GROUPS, S5_STATE), 0.01),
        's5_B_re': nrm(14, (DEPTH, S5_GROUPS, S5_STATE, S5_GROUP), (2 * S5_GROUP) ** -0.5),
        's5_B_im': nrm(15, (DEPTH, S5_GROUPS, S5_STATE, S5_GROUP), (2 * S5_GROUP) ** -0.5),
        's5_C_re': nrm(16, (DEPTH, S5_GROUPS, S5_GROUP, S5_STATE), (2 * S5_STATE) ** -0.5),
        's5_C_im': nrm(17, (DEPTH, S5_GROUPS, S5_GROUP, S5_STATE), (2 * S5_STATE) ** -0.5),
        's5_log_dt': jax.random.uniform(ks[18], (DEPTH, S5_GROUPS), F32, math.log(0.001), math.log(0.1)),
        's5_D': nrm(19, (DEPTH, S5_WIDTH)),
        'w_s5_glu_a': nrm(20, (DEPTH, S5_WIDTH, D_MODEL), S5_WIDTH ** -0.5),
        'w_s5_glu_b': nrm(21, (DEPTH, S5_WIDTH, D_MODEL), S5_WIDTH ** -0.5),
        'gdn_conv_w': nrm(22, (DEPTH, CONV_W, QKV_W), CONV_W ** -0.5),
        'gdn_A_log': jnp.log(jax.random.uniform(ks[23], (DEPTH, GDN_HEADS), F32, 1.0, 16.0)),
        'gdn_dt_bias': dt_gdn + jnp.log(-jnp.expm1(-dt_gdn)),
        'gdn_norm': gain(24, (DEPTH, GDN_DV)),
        'w_gdn_out': nrm(25, (DEPTH, GDN_VW, D_MODEL), GDN_VW ** -0.5),
        'w_out': nrm(26, (DEPTH, D_MODEL, D_MODEL), D_MODEL ** -0.5),
        'w_ffn_gate': nrm(27, (DEPTH, D_MODEL, D_FF), D_MODEL ** -0.5),
        'w_ffn_up': nrm(28, (DEPTH, D_MODEL, D_FF), D_MODEL ** -0.5),
        'w_ffn_down': nrm(29, (DEPTH, D_FF, D_MODEL), D_FF ** -0.5),
    }


def reference(x_prompt, x_sample, state_s5_re, state_s5_im, state_gdn, state_conv, meta_tokens,
              norm_mix_pre, norm_mix_post, norm_ffn_pre, norm_ffn_post, w_in,
              s5_A_re, s5_A_im, s5_B_re, s5_B_im, s5_C_re, s5_C_im, s5_log_dt, s5_D,
              w_s5_glu_a, w_s5_glu_b, gdn_conv_w, gdn_A_log, gdn_dt_bias, gdn_norm, w_gdn_out,
              w_out, w_ffn_gate, w_ffn_up, w_ffn_down):
    xp = jnp.concatenate([jnp.broadcast_to(meta_tokens.astype(x_prompt.dtype)[None], (BATCH, N_META, D_MODEL)),
                          x_prompt], axis=1)
    xs = x_sample
    p_re, p_im, p_gdn, p_conv = [], [], [], []
    s_re, s_im, s_gdn, s_conv = [], [], [], []
    for i in range(DEPTH):
        lp = dict(norm_mix_pre=norm_mix_pre[i], norm_mix_post=norm_mix_post[i],
                  norm_ffn_pre=norm_ffn_pre[i], norm_ffn_post=norm_ffn_post[i], w_in=w_in[i],
                  s5_A_re=s5_A_re[i], s5_A_im=s5_A_im[i], s5_B_re=s5_B_re[i], s5_B_im=s5_B_im[i],
                  s5_C_re=s5_C_re[i], s5_C_im=s5_C_im[i], s5_log_dt=s5_log_dt[i], s5_D=s5_D[i],
                  w_s5_glu_a=w_s5_glu_a[i], w_s5_glu_b=w_s5_glu_b[i], gdn_conv_w=gdn_conv_w[i],
                  gdn_A_log=gdn_A_log[i], gdn_dt_bias=gdn_dt_bias[i], gdn_norm=gdn_norm[i],
                  w_gdn_out=w_gdn_out[i], w_out=w_out[i], w_ffn_gate=w_ffn_gate[i],
                  w_ffn_up=w_ffn_up[i], w_ffn_down=w_ffn_down[i])
        zh = jnp.zeros((BATCH, S5_GROUPS, S5_STATE), F32)
        zs = jnp.zeros((BATCH, GDN_HEADS, GDN_DK, GDN_DV), F32)
        zc = jnp.zeros((BATCH, CONV_W - 1, QKV_W), xp.dtype)
        xp, hr, hi, sg, cb = layer(xp, zh, zh, zs, zc, (N_META,), lp)
        p_re.append(hr.astype(xp.dtype)); p_im.append(hi.astype(xp.dtype))
        p_gdn.append(sg.astype(xp.dtype)); p_conv.append(cb.astype(xp.dtype))
        xs, hr, hi, sg, cb = layer(xs, state_s5_re[i], state_s5_im[i], state_gdn[i], state_conv[i], (), lp)
        s_re.append(hr.astype(state_s5_re.dtype)); s_im.append(hi.astype(state_s5_im.dtype))
        s_gdn.append(sg.astype(state_gdn.dtype)); s_conv.append(cb.astype(state_conv.dtype))
    y_prompt = xp[:, N_META:]
    y_sample = xs
    return (y_prompt, y_sample, jnp.stack(p_re), jnp.stack(p_im), jnp.stack(p_gdn), jnp.stack(p_conv),
            jnp.stack(s_re), jnp.stack(s_im), jnp.stack(s_gdn), jnp.stack(s_conv))
```

```python
import functools
import math

import numpy as np
import jax
import jax.numpy as jnp
from jax import lax
from jax.experimental import pallas as pl
from jax.experimental.pallas import tpu as pltpu

F32 = jnp.float32
BF16 = jnp.bfloat16
EPS = 1e-6
LANES = 128
SUBLANES = 8
CONV_W = 4
CONV_PAD = 8
S5_CLUSTER = 8
S5_SUB_ROWS = 256
VMEM_LIMIT = 62 * 1024 * 1024

NN = (((1,), (0,)), ((), ()))
NT = (((1,), (1,)), ((), ()))
TN = (((0,), (0,)), ((), ()))


def _dot(a, b, dims=NN):
    return lax.dot_general(a.astype(BF16), b.astype(BF16), dims, preferred_element_type=F32)


def _dot_exact(a, b, dims=NN):
    return lax.dot_general(a, b, dims, precision=lax.Precision.HIGHEST, preferred_element_type=F32)


def _split(a):
    hi = a.astype(BF16)
    lo = (a - hi.astype(F32)).astype(BF16)
    return hi, lo


def _dot3(a, b, dims=NN):
    ah, al = _split(a)
    bh, bl = _split(b)
    d = functools.partial(lax.dot_general, dimension_numbers=dims, preferred_element_type=F32)
    return d(ah, bh) + d(al, bh) + d(ah, bl)


def _sigmoid(x):
    return 1.0 / (1.0 + jnp.exp(-x))


def _silu(x):
    return x * _sigmoid(x)


def _softplus(x):
    return jnp.maximum(x, 0.0) + jnp.log1p(jnp.exp(-jnp.abs(x)))


def _gelu_tanh(x):
    c = math.sqrt(2.0 / math.pi)
    return 0.5 * x * (1.0 + jnp.tanh(c * (x + 0.044715 * (x * x * x))))


def _rms(x, g):
    return x * lax.rsqrt(jnp.mean(x * x, axis=-1, keepdims=True) + EPS) * g


def _tri_inverse(m, rt, seg, ri, ci):
    eye = (ri == ci).astype(F32)
    base = min(SUBLANES, seg)
    bsh = int(math.log2(base))
    d = jnp.where((ri >> bsh) == (ci >> bsh), m, 0.0)
    x = eye - d
    p = _dot3(d, d)
    steps = bsh - 1
    for s in range(steps):
        x = x + _dot3(x, p)
        if s + 1 < steps:
            p = _dot3(p, p)
    size = base
    while size < seg:
        sh = int(math.log2(size))
        c = jnp.where(((ri >> (sh + 1)) == (ci >> (sh + 1))) & ((ri >> sh) != (ci >> sh)), m, 0.0)
        x = x - _dot3(x, _dot3(c, x))
        size *= 2
    return x


def _mixer_kernel(cfg, x_ref, h0_ref, s0_ref, c0_ref,
                  gpre_ref, gpost_ref, wu_ref, wqkv_ref, wz_ref, wab_ref, wabt_ref, wg_ref,
                  are_ref, aim_ref, bblk_ref, cblk_ref, dvec_ref, perm_ref, permt_ref,
                  glua_ref, glub_ref, convw_ref, alog_ref, dtb_ref, alogt_ref, dtbt_ref,
                  gnorm_ref, wgo_ref, wout_ref,
                  x1_ref, ht_ref, st_ref, ct_ref,
                  hn_ref, xcat_ref, z_ref, bra_ref, q_ref, k_ref, v_ref, g_ref, beta_ref, gt_ref,
                  o_ref, hs_ref, utm_ref, gstm_ref):
    bb, seg, lr, rt = cfg["bb"], cfg["seg"], cfg["lr"], cfg["rt"]
    d_model, n_heads, dk, qkv_w = cfg["d_model"], cfg["n_heads"], cfg["dk"], cfg["qkv_w"]
    s5w, ns = cfg["s5w"], cfg["ns"]
    r = bb * seg
    n_tm = bb * lr
    kw = n_heads * dk
    t_id = pl.program_id(1)

    @pl.when(t_id == 0)
    def _init():
        ht_ref[...] = h0_ref[...]
        st_ref[...] = s0_ref[...]
        xcat_ref[:, CONV_PAD - (CONV_W - 1):CONV_PAD, :] = c0_ref[...]

    x = x_ref[...].reshape(r, d_model)
    hn = _rms(x, gpre_ref[...]).astype(BF16)
    hn_ref[...] = hn
    u_bf = _dot(hn, wu_ref[...]).astype(BF16)
    xcat_ref[:, CONV_PAD:, :] = _dot(hn, wqkv_ref[...]).reshape(bb, seg, qkv_w)
    z_ref[...] = _dot(hn, wz_ref[...])
    ab = _dot(hn, wab_ref[...])
    abt = lax.dot_general(wabt_ref[...], hn, NT, preferred_element_type=F32)

    utm_ref[...] = _dot(perm_ref[...], u_bf).astype(BF16)
    sub = min(S5_SUB_ROWS, n_tm)
    ts = sub // bb
    n_clusters = s5w // LANES
    cst = ns // n_clusters
    for sb in range(n_tm // sub):
        u_sb = utm_ref[sb * sub:(sb + 1) * sub, :]
        for j in range(n_clusters):
            bu = _dot(u_sb[:, j * LANES:(j + 1) * LANES], bblk_ref[j])
            hs_ref[:, j * cst:(j + 1) * cst] = bu[:, :cst]
            hs_ref[:, ns + j * cst:ns + (j + 1) * cst] = bu[:, cst:]
        half = ns // 2
        for c in range(2):
            lo = c * half
            a_re = jnp.broadcast_to(are_ref[:, lo:lo + half], (bb, half))
            a_im = jnp.broadcast_to(aim_ref[:, lo:lo + half], (bb, half))

            def step(t, carry, lo=lo, a_re=a_re, a_im=a_im):
                h_re, h_im = carry
                rows = pl.ds(pl.multiple_of(t * bb, SUBLANES), bb)
                n_re = a_re * h_re - a_im * h_im + hs_ref[rows, lo:lo + half]
                n_im = a_re * h_im + a_im * h_re + hs_ref[rows, ns + lo:ns + lo + half]
                hs_ref[rows, lo:lo + half] = n_re
                hs_ref[rows, ns + lo:ns + lo + half] = n_im
                return n_re, n_im

            h_re, h_im = lax.fori_loop(0, ts, step,
                                       (ht_ref[:, lo:lo + half], ht_ref[:, ns + lo:ns + lo + half]))
            ht_ref[:, lo:lo + half] = h_re
            ht_ref[:, ns + lo:ns + lo + half] = h_im
        ys = []
        for j in range(n_clusters):
            hj = jnp.concatenate([hs_ref[:, j * cst:(j + 1) * cst],
                                  hs_ref[:, ns + j * cst:ns + (j + 1) * cst]], axis=1)
            ys.append(_dot(hj, cblk_ref[j]))
        y = jnp.concatenate(ys, axis=1) + dvec_ref[...] * u_sb.astype(F32)
        gstm_ref[sb * sub:(sb + 1) * sub, :] = _gelu_tanh(y).astype(BF16)
    gs = _dot(permt_ref[...], gstm_ref[...]).astype(BF16)
    bra_ref[...] = _dot(gs, glua_ref[...]) * _sigmoid(_dot(gs, glub_ref[...]))

    base = CONV_PAD - (CONV_W - 1)
    acc = convw_ref[0:1, :] * xcat_ref[:, base:base + seg, :]
    for j in range(1, CONV_W):
        acc = acc + convw_ref[j:j + 1, :] * xcat_ref[:, base + j:base + j + seg, :]
    new_hist = xcat_ref[:, base + lr:base + lr + (CONV_W - 1), :]
    xcat_ref[:, base:CONV_PAD, :] = new_hist
    qkv_c = _silu(acc).reshape(r, qkv_w)
    if lr < seg:
        row_t = lax.broadcasted_iota(jnp.int32, (r, 1), 0) & (seg - 1)
        real = row_t < lr
        qkv_c = jnp.where(real, qkv_c, 0.0)
    for h in range(n_heads):
        hs = slice(h * dk, (h + 1) * dk)
        qh = qkv_c[:, h * dk:(h + 1) * dk]
        kh = qkv_c[:, kw + h * dk:kw + (h + 1) * dk]
        q_ref[:, hs] = qh * lax.rsqrt(jnp.sum(qh * qh, axis=-1, keepdims=True) + EPS) * (dk ** -0.5)
        k_ref[:, hs] = kh * lax.rsqrt(jnp.sum(kh * kh, axis=-1, keepdims=True) + EPS)
    v_ref[...] = qkv_c[:, 2 * kw:]

    g_full = -jnp.exp(alog_ref[...]) * _softplus(ab + dtb_ref[...])
    b_full = _sigmoid(ab)
    g_t = -jnp.exp(alogt_ref[...]) * _softplus(abt + dtbt_ref[...])
    if lr < seg:
        g_full = jnp.where(real, g_full, 0.0)
        col_t = lax.broadcasted_iota(jnp.int32, (1, r), 1) & (seg - 1)
        g_t = jnp.where(col_t < lr, g_t, 0.0)
    e_row = lax.broadcasted_iota(jnp.int32, (SUBLANES, n_heads * LANES), 0)
    e_col = lax.broadcasted_iota(jnp.int32, (SUBLANES, n_heads * LANES), 1) >> 7
    g_ref[...] = _dot_exact(g_full[:, :SUBLANES], (e_row == e_col).astype(F32))
    beta_ref[...] = _dot_exact(b_full[:, :SUBLANES], (e_row == e_col + n_heads).astype(F32))
    n_tiles = r // rt
    for i in range(n_tiles):
        gt_ref[i] = g_t[:, i * rt:(i + 1) * rt]

    nseq = rt // seg
    ssh = int(math.log2(seg))
    ri = lax.broadcasted_iota(jnp.int32, (rt, rt), 0)
    ci = lax.broadcasted_iota(jnp.int32, (rt, rt), 1)
    same = (ri >> ssh) == (ci >> ssh)
    causal = same & (ri >= ci)
    strict = same & (ri > ci)
    lmask = causal.astype(F32)
    umask = (same & (ri <= ci)).astype(F32)
    smask = same.astype(F32)

    def tile_body(i, carry):
        rows = pl.ds(pl.multiple_of(i * rt, rt), rt)
        g_tile = g_ref[rows, :]
        gc_all = _dot_exact(lmask, g_tile)
        gl_all = _dot_exact(smask, g_tile)
        gct = _dot_exact(gt_ref[i], umask)
        for h in range(n_heads):
            hs = slice(h * LANES, (h + 1) * LANES)
            gc = gc_all[:, hs]
            diff = gc[:, :rt] - gct[h:h + 1, :]
            decay = jnp.exp(jnp.where(causal, diff, -1e30))
            qh = q_ref[rows, hs]
            kh = k_ref[rows, hs]
            vh = v_ref[rows, hs]
            beta = beta_ref[rows, hs]
            kb = kh * beta
            m = jnp.where(strict, _dot3(kb, kh, NT) * decay, 0.0)
            tinv = _tri_inverse(m, rt, seg, ri, ci)
            egc = jnp.exp(gc)
            uw = _dot3(tinv, jnp.concatenate([vh * beta, kb * egc], axis=1))
            u_, w_ = uw[:, :LANES], uw[:, LANES:]
            qk = _dot3(qh, kh, NT) * decay
            qd = qh * egc
            gl = gl_all[:, hs]
            kd = kh * jnp.exp(gl - gc)
            ws, qs, states = [], [], []
            for s in range(nseq):
                rs = slice(s * seg, (s + 1) * seg)
                st = st_ref[i * nseq + s, h]
                res = _dot3(jnp.concatenate([w_[rs], qd[rs]], axis=0), st)
                ws.append(res[:seg])
                qs.append(res[seg:])
                states.append(st)
            v_new = u_ - jnp.concatenate(ws, axis=0)
            o_ref[rows, hs] = jnp.concatenate(qs, axis=0) + _dot3(qk, v_new)
            for s in range(nseq):
                rs = slice(s * seg, (s + 1) * seg)
                dec = jnp.exp(gl[s * seg:s * seg + 1, :])
                st_ref[i * nseq + s, h] = states[s] * dec + _dot3(kd[rs], v_new[rs], TN)
        return carry

    lax.fori_loop(0, n_tiles, tile_body, 0)

    z = z_ref[...]
    og = []
    for h in range(n_heads):
        hs = slice(h * LANES, (h + 1) * LANES)
        oh = o_ref[:, hs]
        og.append(oh * lax.rsqrt(jnp.mean(oh * oh, axis=-1, keepdims=True) + EPS))
    o_n = jnp.concatenate(og, axis=1) * gnorm_ref[...] * _silu(z)
    br_b = _dot(o_n, wgo_ref[...])
    gates = _dot(hn_ref[...], wg_ref[...])
    mix = _sigmoid(gates[:, :d_model]) * bra_ref[...] + _sigmoid(gates[:, d_model:]) * br_b
    x1 = x + _rms(_dot(mix, wout_ref[...]), gpost_ref[...])
    x1_ref[...] = x1.reshape(bb, seg, d_model)

    @pl.when(t_id == pl.num_programs(1) - 1)
    def _fin():
        ct_ref[...] = xcat_ref[:, CONV_PAD - (CONV_W - 1):CONV_PAD, :]


def _mixer_call(x, h0, s0, c0, w, *, bb, seg, lr, rt):
    b_total, l_total, d_model = x.shape
    n_heads, dk, dv = s0.shape[1:]
    qkv_w = c0.shape[2]
    s5w = w["wu"].shape[1]
    ns = w["are"].shape[1]
    assert b_total % bb == 0 and l_total % seg == 0 and bb % SUBLANES == 0
    assert seg & (seg - 1) == 0 and rt % seg == 0 and (bb * seg) % rt == 0 and seg % SUBLANES == 0
    assert dk == LANES and dv == LANES and lr >= CONV_W - 1
    r = bb * seg
    n_tm = bb * lr
    sub = min(S5_SUB_ROWS, n_tm)
    assert n_tm % sub == 0 and sub % bb == 0
    nb, nt = b_total // bb, l_total // seg
    assert nt == 1 or lr == seg

    perm = np.zeros((n_tm, r), np.float32)
    for t in range(lr):
        for b in range(bb):
            perm[t * bb + b, b * seg + t] = 1.0
    perm_j = jnp.asarray(perm, BF16)
    permt_j = jnp.asarray(perm.T, BF16)
    alogt = jnp.broadcast_to(w["alog_col"], (2 * SUBLANES, r))
    dtbt = jnp.broadcast_to(w["dtb_col"], (2 * SUBLANES, r))

    cfg = dict(bb=bb, seg=seg, lr=lr, rt=rt, d_model=d_model, n_heads=n_heads, dk=dk, qkv_w=qkv_w, s5w=s5w, ns=ns)
    weights = [w["gpre"], w["gpost"], w["wu"], w["wqkv"], w["wz"], w["wab"], w["wabt"], w["wg"],
               w["are"], w["aim"], w["bblk"], w["cblk"], w["dvec"], perm_j, permt_j,
               w["glua"], w["glub"], w["convw"], w["alog_row"], w["dtb_row"], alogt, dtbt,
               w["gnorm"], w["wgo"], w["wout"]]
    vmem_spec = pl.BlockSpec(memory_space=pltpu.VMEM)
    in_specs = [
        pl.BlockSpec((bb, seg, d_model), lambda b, t: (b, t, 0)),
        pl.BlockSpec((bb, 2 * ns), lambda b, t: (b, 0)),
        pl.BlockSpec((bb, n_heads, dk, dv), lambda b, t: (b, 0, 0, 0)),
        pl.BlockSpec((bb, CONV_W - 1, qkv_w), lambda b, t: (b, 0, 0)),
    ] + [vmem_spec] * len(weights)
    out_shape = (
        jax.ShapeDtypeStruct((b_total, l_total, d_model), F32),
        jax.ShapeDtypeStruct((b_total, 2 * ns), F32),
        jax.ShapeDtypeStruct((b_total, n_heads, dk, dv), F32),
        jax.ShapeDtypeStruct((b_total, CONV_W - 1, qkv_w), F32),
    )
    out_specs = (
        pl.BlockSpec((bb, seg, d_model), lambda b, t: (b, t, 0)),
        pl.BlockSpec((bb, 2 * ns), lambda b, t: (b, 0)),
        pl.BlockSpec((bb, n_heads, dk, dv), lambda b, t: (b, 0, 0, 0)),
        pl.BlockSpec((bb, CONV_W - 1, qkv_w), lambda b, t: (b, 0, 0)),
    )
    hw = n_heads * LANES
    scratch = [
        pltpu.VMEM((r, d_model), BF16),
        pltpu.VMEM((bb, CONV_PAD + seg, qkv_w), F32),
        pltpu.VMEM((r, hw), F32),
        pltpu.VMEM((r, d_model), F32),
        pltpu.VMEM((r, hw), F32),
        pltpu.VMEM((r, hw), F32),
        pltpu.VMEM((r, hw), F32),
        pltpu.VMEM((r, hw), F32),
        pltpu.VMEM((r, hw), F32),
        pltpu.VMEM((r // rt, 2 * SUBLANES, rt), F32),
        pltpu.VMEM((r, hw), F32),
        pltpu.VMEM((sub, 2 * ns), F32),
        pltpu.VMEM((n_tm, s5w), BF16),
        pltpu.VMEM((n_tm, s5w), BF16),
    ]
    return pl.pallas_call(
        functools.partial(_mixer_kernel, cfg),
        grid=(nb, nt),
        in_specs=in_specs,
        out_specs=out_specs,
        out_shape=out_shape,
        scratch_shapes=scratch,
        compiler_params=pltpu.CompilerParams(
            dimension_semantics=("arbitrary", "arbitrary"), vmem_limit_bytes=VMEM_LIMIT),
        name=f"mixer_bb{bb}_seg{seg}",
    )(x, h0, s0, c0, *weights)


def _ffn_kernel(x_ref, gpre_ref, gpost_ref, wg_ref, wu_ref, wd_ref, y_ref):
    x = x_ref[...]
    f = _rms(x, gpre_ref[...]).astype(BF16)
    hid = _silu(_dot(f, wg_ref[...])) * _dot(f, wu_ref[...])
    y_ref[...] = x + _rms(_dot(hid, wd_ref[...]), gpost_ref[...])


def _ffn_call(x, w, *, tm):
    n, d_model = x.shape
    assert n % tm == 0
    vmem_spec = pl.BlockSpec(memory_space=pltpu.VMEM)
    return pl.pallas_call(
        _ffn_kernel,
        grid=(n // tm,),
        in_specs=[pl.BlockSpec((tm, d_model), lambda i: (i, 0))] + [vmem_spec] * 5,
        out_specs=pl.BlockSpec((tm, d_model), lambda i: (i, 0)),
        out_shape=jax.ShapeDtypeStruct((n, d_model), F32),
        compiler_params=pltpu.CompilerParams(
            dimension_semantics=("arbitrary",), vmem_limit_bytes=VMEM_LIMIT),
        name=f"ffn_tm{tm}",
    )(x, w["f_gpre"], w["f_gpost"], w["f_wg"], w["f_wu"], w["f_wd"])


def _prepare_weights(norm_mix_pre, norm_mix_post, norm_ffn_pre, norm_ffn_post, w_in,
                     a_re, a_im, b_re, b_im, c_re, c_im, log_dt, d_skip,
                     w_glu_a, w_glu_b, conv_w, a_log, dt_bias, gdn_norm, w_gdn_out,
                     w_out, w_ffn_gate, w_ffn_up, w_ffn_down, *, n_heads, dk, dv):
    d_model = w_in.shape[0]
    groups, n_state = a_re.shape
    gsz = b_re.shape[2]
    s5w = groups * gsz
    kw, vw = n_heads * dk, n_heads * dv
    o1 = s5w
    o2 = o1 + 2 * kw + vw
    o3 = o2 + vw
    o4 = o3 + n_heads
    o5 = o4 + n_heads
    w = {}
    w["gpre"] = norm_mix_pre.reshape(1, d_model)
    w["gpost"] = norm_mix_post.reshape(1, d_model)
    w["wu"] = w_in[:, :o1].astype(BF16)
    w["wqkv"] = w_in[:, o1:o2].astype(BF16)
    w["wz"] = w_in[:, o2:o3].astype(BF16)
    wab = jnp.pad(w_in[:, o3:o5], ((0, 0), (0, LANES - 2 * n_heads)))
    w["wab"] = wab.astype(BF16)
    w["wabt"] = wab[:, :2 * SUBLANES].T.astype(BF16)
    w["wg"] = w_in[:, o5:].astype(BF16)

    dt = jnp.exp(log_dt)[:, None]
    mag = jnp.exp(a_re * dt)
    ar = mag * jnp.cos(a_im * dt)
    ai = mag * jnp.sin(a_im * dt)
    den = a_re * a_re + a_im * a_im
    fr = ((ar - 1.0) * a_re + ai * a_im) / den
    fi = (ai * a_re - (ar - 1.0) * a_im) / den
    bbr = fr[..., None] * b_re - fi[..., None] * b_im
    bbi = fr[..., None] * b_im + fi[..., None] * b_re
    ncl = groups // S5_CLUSTER
    eye = jnp.eye(S5_CLUSTER, dtype=F32)
    cst = S5_CLUSTER * n_state

    def in_blk(t):
        t4 = t.reshape(ncl, S5_CLUSTER, n_state, gsz)
        return jnp.einsum("jgpc,gh->jgchp", t4, eye).reshape(ncl, S5_CLUSTER * gsz, cst)

    def out_blk(t):
        t4 = t.reshape(ncl, S5_CLUSTER, gsz, n_state)
        return jnp.einsum("jgcp,gh->jhpgc", t4, eye).reshape(ncl, cst, S5_CLUSTER * gsz)

    w["are"] = ar.reshape(1, groups * n_state)
    w["aim"] = ai.reshape(1, groups * n_state)
    w["bblk"] = jnp.concatenate([in_blk(bbr), in_blk(bbi)], axis=2).astype(BF16)
    w["cblk"] = jnp.concatenate([out_blk(c_re), out_blk(-c_im)], axis=1).astype(BF16)
    w["dvec"] = d_skip.reshape(1, s5w)
    w["glua"] = w_glu_a.astype(BF16)
    w["glub"] = w_glu_b.astype(BF16)
    w["convw"] = conv_w
    pad_h = LANES - n_heads
    w["alog_row"] = jnp.pad(a_log, (0, pad_h)).reshape(1, LANES)
    w["dtb_row"] = jnp.pad(dt_bias, (0, pad_h)).reshape(1, LANES)
    w["alog_col"] = jnp.pad(a_log, (0, 2 * SUBLANES - n_heads)).reshape(2 * SUBLANES, 1)
    w["dtb_col"] = jnp.pad(dt_bias, (0, 2 * SUBLANES - n_heads)).reshape(2 * SUBLANES, 1)
    w["gnorm"] = jnp.tile(gdn_norm, n_heads).reshape(1, n_heads * dv)
    w["wgo"] = w_gdn_out.astype(BF16)
    w["wout"] = w_out.astype(BF16)
    w["f_gpre"] = norm_ffn_pre.reshape(1, d_model)
    w["f_gpost"] = norm_ffn_post.reshape(1, d_model)
    w["f_wg"] = w_ffn_gate.astype(BF16)
    w["f_wu"] = w_ffn_up.astype(BF16)
    w["f_wd"] = w_ffn_down.astype(BF16)
    return w


def _pick_ffn_tile(n):
    for tm in (512, 256, 128, 64, 32, 16, 8):
        if n % tm == 0:
            return tm
    raise ValueError(n)


def kernel(x_prompt, x_sample, state_s5_re, state_s5_im, state_gdn, state_conv, meta_tokens, norm_mix_pre, norm_mix_post, norm_ffn_pre, norm_ffn_post, w_in, s5_A_re, s5_A_im, s5_B_re, s5_B_im, s5_C_re, s5_C_im, s5_log_dt, s5_D, w_s5_glu_a, w_s5_glu_b, gdn_conv_w, gdn_A_log, gdn_dt_bias, gdn_norm, w_gdn_out, w_out, w_ffn_gate, w_ffn_up, w_ffn_down):
    depth = w_in.shape[0]
    assert depth == 1, "single-layer step"
    batch, seq, d_model = x_prompt.shape
    dec_batch, dec_seq, _ = x_sample.shape
    n_meta = meta_tokens.shape[0]
    _, _, n_heads, dk, dv = state_gdn.shape
    groups, n_state = s5_A_re.shape[1:]
    ns = groups * n_state
    qkv_w = state_conv.shape[3]

    w = _prepare_weights(norm_mix_pre[0], norm_mix_post[0], norm_ffn_pre[0], norm_ffn_post[0], w_in[0],
                         s5_A_re[0], s5_A_im[0], s5_B_re[0], s5_B_im[0], s5_C_re[0], s5_C_im[0],
                         s5_log_dt[0], s5_D[0], w_s5_glu_a[0], w_s5_glu_b[0], gdn_conv_w[0],
                         gdn_A_log[0], gdn_dt_bias[0], gdn_norm[0], w_gdn_out[0], w_out[0],
                         w_ffn_gate[0], w_ffn_up[0], w_ffn_down[0], n_heads=n_heads, dk=dk, dv=dv)

    mb = SUBLANES
    x_meta = jnp.pad(meta_tokens[None], ((0, mb - 1), (0, 0), (0, 0)))
    _, m_h, m_s, m_c = _mixer_call(
        x_meta, jnp.zeros((mb, 2 * ns), F32), jnp.zeros((mb, n_heads, dk, dv), F32),
        jnp.zeros((mb, CONV_W - 1, qkv_w), F32), w, bb=mb, seg=n_meta, lr=n_meta, rt=n_meta)

    p_seg = 64
    xp1, p_h, p_s, p_c = _mixer_call(
        x_prompt, jnp.broadcast_to(m_h[:1], (batch, 2 * ns)),
        jnp.broadcast_to(m_s[:1], (batch, n_heads, dk, dv)),
        jnp.broadcast_to(m_c[:1], (batch, CONV_W - 1, qkv_w)), w, bb=batch, seg=p_seg, lr=p_seg, rt=p_seg)
    y_prompt = _ffn_call(xp1.reshape(batch * seq, d_model), w, tm=_pick_ffn_tile(batch * seq))
    y_prompt = y_prompt.reshape(batch, seq, d_model)

    s_seg = SUBLANES
    assert dec_seq <= s_seg
    xs = jnp.pad(x_sample, ((0, 0), (0, s_seg - dec_seq), (0, 0)))
    h0 = jnp.concatenate([state_s5_re[0].reshape(dec_batch, ns), state_s5_im[0].reshape(dec_batch, ns)], axis=1)
    s_bb = 16
    xs1, s_h, s_s, s_c = _mixer_call(xs, h0, state_gdn[0], state_conv[0], w,
                                     bb=s_bb, seg=s_seg, lr=dec_seq, rt=s_bb * s_seg)
    xs1 = xs1[:, :dec_seq].reshape(dec_batch * dec_seq, d_model)
    y_sample = _ffn_call(xs1, w, tm=_pick_ffn_tile(dec_batch * dec_seq)).reshape(dec_batch, dec_seq, d_model)

    def s5_out(h, b):
        return h[:, :ns].reshape(1, b, groups, n_state), h[:, ns:].reshape(1, b, groups, n_state)

    p_re, p_im = s5_out(p_h, batch)
    s_re, s_im = s5_out(s_h, dec_batch)
    return (y_prompt, y_sample, p_re, p_im, p_s[None], p_c[None], s_re, s_im, s_s[None], s_c[None])
```
